```python
import math
import jax, jax.numpy as jnp
from jax import lax
import numpy as np

D_MODEL = 1024
BATCH = 8
SEQ = 4096
DEPTH = 1

MLA_HEADS = 16
MLA_NOPE = 64
MLA_ROPE = 32
MLA_V = 64
Q_LORA = 256
KV_LORA = 128
SWA_Q_HEADS = 16
SWA_KV_HEADS = 2
SWA_GROUP = SWA_Q_HEADS // SWA_KV_HEADS
SWA_HEAD_DIM = 64
WINDOW = 128
Q_BLOCK = 128
ROPE_THETA = 10000.0
D_FF = 4 * D_MODEL
PLE_DIM = 256
N_BRANCHES = 2
NORM_EPS = 1e-6
NEG = -1e30

IN_SPLITS = (Q_LORA, KV_LORA, MLA_ROPE,
             SWA_Q_HEADS * SWA_HEAD_DIM, SWA_KV_HEADS * SWA_HEAD_DIM, SWA_KV_HEADS * SWA_HEAD_DIM,
             N_BRANCHES * D_MODEL)
IN_COLS = sum(IN_SPLITS)
IN_OFFSETS = tuple(int(v) for v in np.cumsum(IN_SPLITS)[:-1])

kernel_name = "hybrid_mla_swa_gated_sandwich"


def rms_norm(x, g):
    xf = x.astype(jnp.float32)
    y = xf * lax.rsqrt(jnp.mean(xf * xf, axis=-1, keepdims=True) + NORM_EPS)
    return (y * g.astype(jnp.float32)).astype(x.dtype)


def rope(x, pos):
    d = x.shape[-1]
    half = d // 2
    inv = jnp.exp(-math.log(ROPE_THETA) * jnp.arange(half, dtype=jnp.float32) * (2.0 / d))
    ang = pos.astype(jnp.float32)[:, None] * inv[None, :]
    cos = jnp.cos(ang)[None, :, None, :]
    sin = jnp.sin(ang)[None, :, None, :]
    xf = x.astype(jnp.float32)
    x1, x2 = xf[..., :half], xf[..., half:]
    return jnp.concatenate([x1 * cos - x2 * sin, x2 * cos + x1 * sin], axis=-1).astype(x.dtype)


def mla_attention(q, k, v):
    B, S, H, D = q.shape
    nb = S // Q_BLOCK
    scale = D ** -0.5
    qb = q.reshape(B, nb, Q_BLOCK, H, D).transpose(1, 0, 2, 3, 4)
    kpos = jnp.arange(S)

    def one_block(args):
        qblk, i = args
        s = jnp.einsum('bqhd,bkhd->bhqk', qblk, k, preferred_element_type=jnp.float32) * scale
        qpos = i * Q_BLOCK + jnp.arange(Q_BLOCK)
        mask = kpos[None, :] <= qpos[:, None]
        s = jnp.where(mask[None, None], s, NEG)
        pr = jax.nn.softmax(s, axis=-1)
        return jnp.einsum('bhqk,bkhd->bqhd', pr.astype(v.dtype), v)

    out = lax.map(one_block, (qb, jnp.arange(nb)))
    return out.transpose(1, 0, 2, 3, 4).reshape(B, S, H, v.shape[-1])


def swa_attention(q, k, v, sinks):
    B, S, HQ, hd = q.shape
    blk = WINDOW
    nb = S // blk
    scale = hd ** -0.5
    qb = q.reshape(B, nb, blk, SWA_KV_HEADS, SWA_GROUP, hd)

    def band(t):
        tb = t.reshape(B, nb, blk, SWA_KV_HEADS, hd)
        prev = jnp.pad(tb[:, :-1], ((0, 0), (1, 0), (0, 0), (0, 0), (0, 0)))
        return jnp.concatenate([prev, tb], axis=2)

    kb, vb = band(k), band(v)
    s = jnp.einsum('bnqhgd,bnkhd->bnhgqk', qb, kb, preferred_element_type=jnp.float32) * scale
    qi = jnp.arange(blk)[:, None]
    kj = jnp.arange(2 * blk)[None, :] - blk
    rel = qi - kj
    band_mask = (rel >= 0) & (rel < WINDOW)
    valid = (jnp.arange(nb)[:, None, None] * blk + kj[None]) >= 0
    mask = band_mask[None] & valid
    s = jnp.where(mask[None, :, None, None], s, NEG)
    sink = sinks.astype(jnp.float32).reshape(SWA_KV_HEADS, SWA_GROUP)[None, None, :, :, None, None]
    m = jnp.maximum(jnp.max(s, axis=-1, keepdims=True), sink)
    e = jnp.exp(s - m)
    pr = e / (jnp.sum(e, axis=-1, keepdims=True) + jnp.exp(sink - m))
    o = jnp.einsum('bnhgqk,bnkhd->bnqhgd', pr.astype(v.dtype), vb)
    return o.reshape(B, S, HQ, hd)


def setup_inputs(seed: int = 0) -> dict:
    key = jax.random.key(seed)
    ks = jax.random.split(key, 24)

    def dense(k, fan_in, fan_out):
        return jax.random.normal(k, (DEPTH, fan_in, fan_out), jnp.float32) * fan_in ** -0.5

    def gain(k, n):
        return 1.0 + 0.05 * jax.random.normal(k, (DEPTH, n), jnp.float32)

    return {
        "x": jax.random.normal(ks[0], (BATCH, SEQ, D_MODEL), jnp.float32),
        "p": jax.random.normal(ks[1], (DEPTH, BATCH, SEQ, PLE_DIM), jnp.float32),
        "g_mix_pre": gain(ks[2], D_MODEL),
        "w_in": dense(ks[3], D_MODEL, IN_COLS),
        "g_q_a": gain(ks[4], Q_LORA),
        "w_q_b": dense(ks[5], Q_LORA, MLA_HEADS * (MLA_NOPE + MLA_ROPE)),
        "g_kv_a": gain(ks[6], KV_LORA),
        "w_kv_b": dense(ks[7], KV_LORA, MLA_HEADS * (MLA_NOPE + MLA_V)),
        "sinks": 0.5 * jax.random.normal(ks[8], (DEPTH, SWA_Q_HEADS), jnp.float32),
        "w_mla_up": dense(ks[9], MLA_HEADS * MLA_V, D_MODEL),
        "w_swa_up": dense(ks[10], SWA_Q_HEADS * SWA_HEAD_DIM, D_MODEL),
        "w_out": dense(ks[11], D_MODEL, D_MODEL),
        "g_mix_post": gain(ks[12], D_MODEL),
        "g_mlp_pre": gain(ks[13], D_MODEL),
        "w_mlp_up": dense(ks[14], D_MODEL, D_FF),
        "w_mlp_down": dense(ks[15], D_FF, D_MODEL),
        "g_mlp_post": gain(ks[16], D_MODEL),
        "w_ple": dense(ks[17], PLE_DIM, D_MODEL),
        "g_ple": gain(ks[18], D_MODEL),
        "w_ple_gate": dense(ks[19], D_MODEL, D_MODEL),
    }


def reference(x, p, g_mix_pre, w_in, g_q_a, w_q_b, g_kv_a, w_kv_b, sinks, w_mla_up, w_swa_up,
              w_out, g_mix_post, g_mlp_pre, w_mlp_up, w_mlp_down, g_mlp_post, w_ple, g_ple,
              w_ple_gate):
    B, S, _ = x.shape
    pos = jnp.arange(S)
    for i in range(DEPTH):
        h = rms_norm(x, g_mix_pre[i])
        z = h @ w_in[i]
        q_a, kv_a, k_r, sq, sk, sv, gates = jnp.split(z, IN_OFFSETS, axis=-1)

        qm = (rms_norm(q_a, g_q_a[i]) @ w_q_b[i]).reshape(B, S, MLA_HEADS, MLA_NOPE + MLA_ROPE)
        q_nope, q_rope = qm[..., :MLA_NOPE], rope(qm[..., MLA_NOPE:], pos)
        kvm = (rms_norm(kv_a, g_kv_a[i]) @ w_kv_b[i]).reshape(B, S, MLA_HEADS, MLA_NOPE + MLA_V)
        k_nope, v_m = kvm[..., :MLA_NOPE], kvm[..., MLA_NOPE:]
        k_rope = jnp.broadcast_to(rope(k_r[:, :, None, :], pos), (B, S, MLA_HEADS, MLA_ROPE))
        q_mla = jnp.concatenate([q_nope, q_rope], axis=-1)
        k_mla = jnp.concatenate([k_nope, k_rope], axis=-1)
        o_mla = mla_attention(q_mla, k_mla, v_m).reshape(B, S, MLA_HEADS * MLA_V)

        q_s = rope(sq.reshape(B, S, SWA_Q_HEADS, SWA_HEAD_DIM), pos)
        k_s = rope(sk.reshape(B, S, SWA_KV_HEADS, SWA_HEAD_DIM), pos)
        v_s = sv.reshape(B, S, SWA_KV_HEADS, SWA_HEAD_DIM)
        o_swa = swa_attention(q_s, k_s, v_s, sinks[i]).reshape(B, S, SWA_Q_HEADS * SWA_HEAD_DIM)

        g_a, g_b = gates[..., :D_MODEL], gates[..., D_MODEL:]
        y = jax.nn.sigmoid(g_a) * (o_mla @ w_mla_up[i]) + jax.nn.sigmoid(g_b) * (o_swa @ w_swa_up[i])
        x = x + rms_norm(y @ w_out[i], g_mix_post[i])

        h = rms_norm(x, g_mlp_pre[i])
        u = jnp.square(jax.nn.relu(h @ w_mlp_up[i]))
        x = x + rms_norm(u @ w_mlp_down[i], g_mlp_post[i])

        e = rms_norm(p[i] @ w_ple[i], g_ple[i])
        x = x + jax.nn.sigmoid(x @ w_ple_gate[i]) * e
    return x
```

```python
import functools
import math

import numpy as np
import jax
import jax.numpy as jnp
from jax import lax
from jax.experimental import pallas as pl
from jax.experimental.pallas import tpu as pltpu

D_MODEL = 1024
MLA_HEADS = 16
MLA_NOPE = 64
MLA_ROPE = 32
MLA_V = 64
Q_LORA = 256
KV_LORA = 128
SWA_Q_HEADS = 16
SWA_KV_HEADS = 2
SWA_GROUP = SWA_Q_HEADS // SWA_KV_HEADS
SWA_HEAD_DIM = 64
WINDOW = 128
ROPE_THETA = 10000.0
D_FF = 4 * D_MODEL
PLE_DIM = 256
NORM_EPS = 1e-6
NEG = -1e30

LANES = 128
HALF = LANES // 2

IN_SPLITS = (Q_LORA, KV_LORA, MLA_ROPE,
             SWA_Q_HEADS * SWA_HEAD_DIM, SWA_KV_HEADS * SWA_HEAD_DIM,
             SWA_KV_HEADS * SWA_HEAD_DIM, 2 * D_MODEL)
IN_COLS = sum(IN_SPLITS)
IN_OFFSETS = tuple(int(v) for v in np.cumsum(IN_SPLITS)[:-1])

C_QA = 0
C_KVA = C_QA + Q_LORA
C_KR = C_KVA + KV_LORA
C_SQ = C_KR + LANES
C_SK = C_SQ + SWA_Q_HEADS * SWA_HEAD_DIM
C_SV = C_SK + SWA_KV_HEADS * LANES
C_GATE = C_SV + SWA_KV_HEADS * LANES
C_END = C_GATE + 2 * D_MODEL

VMEM_LIMIT = 56 * 1024 * 1024

TM_PROJ = 512
TQ_MLA = 256
TM_MERGE = 512
TM_MLP = 512
FF_CHUNK = 1024


def _in_proj_columns():
    z = IN_COLS
    o_qa, o_kva, o_kr, o_sq, o_sk, o_sv, o_g = (0,) + IN_OFFSETS
    idx = list(range(o_qa, o_qa + Q_LORA))
    idx += list(range(o_kva, o_kva + KV_LORA))
    r = MLA_ROPE // 2
    idx += ([o_kr + j for j in range(r)] + [z] * (HALF - r)
            + [o_kr + r + j for j in range(r)] + [z] * (HALF - r))
    hd = SWA_HEAD_DIM // 2
    for pair in range(SWA_Q_HEADS // 2):
        b0 = o_sq + (2 * pair) * SWA_HEAD_DIM
        b1 = b0 + SWA_HEAD_DIM
        idx += ([b0 + j for j in range(hd)] + [b1 + j for j in range(hd)]
                + [b0 + hd + j for j in range(hd)] + [b1 + hd + j for j in range(hd)])
    for g in range(SWA_KV_HEADS):
        b = o_sk + g * SWA_HEAD_DIM
        x1 = [b + j for j in range(hd)]
        x2 = [b + hd + j for j in range(hd)]
        idx += x1 + x1 + x2 + x2
    for g in range(SWA_KV_HEADS):
        b = o_sv + g * SWA_HEAD_DIM
        idx += [b + j for j in range(SWA_HEAD_DIM)] + [z] * (LANES - SWA_HEAD_DIM)
    idx += list(range(o_g, o_g + 2 * D_MODEL))
    assert len(idx) == C_END
    return np.asarray(idx, np.int32)


def _mla_head_layout(nope_cols, rope_cols, zero):
    r = MLA_ROPE // 2
    x1 = rope_cols[:r] if rope_cols is not None else [zero] * r
    x2 = rope_cols[r:] if rope_cols is not None else [zero] * r
    a = HALF - r
    return (x1 + nope_cols[:a] + x2 + nope_cols[a:]
            + [zero] * (LANES - MLA_NOPE - MLA_ROPE))


def _q_up_columns():
    z = MLA_HEADS * (MLA_NOPE + MLA_ROPE)
    idx = []
    for h in range(MLA_HEADS):
        b = h * (MLA_NOPE + MLA_ROPE)
        idx += _mla_head_layout([b + j for j in range(MLA_NOPE)],
                                [b + MLA_NOPE + j for j in range(MLA_ROPE)], z)
    return np.asarray(idx, np.int32)


def _kv_up_columns():
    z = MLA_HEADS * (MLA_NOPE + MLA_V)
    kidx, vidx = [], []
    for h in range(MLA_HEADS):
        b = h * (MLA_NOPE + MLA_V)
        kidx += _mla_head_layout([b + j for j in range(MLA_NOPE)], None, z)
        vidx += [b + MLA_NOPE + j for j in range(MLA_V)] + [z] * (LANES - MLA_V)
    return np.asarray(kidx + vidx, np.int32)


def _take_cols(w, idx):
    w = jnp.concatenate([w, jnp.zeros((w.shape[0], 1), w.dtype)], axis=1)
    return jnp.take(w, jnp.asarray(idx), axis=1).astype(jnp.bfloat16)


def _rope_tables(seq):
    pos = jnp.arange(seq, dtype=jnp.float32)[:, None]

    def cs(d):
        half = d // 2
        inv = jnp.exp(-math.log(ROPE_THETA) * jnp.arange(half, dtype=jnp.float32) * (2.0 / d))
        ang = pos * inv[None, :]
        return jnp.cos(ang), jnp.sin(ang)

    cm, sm = cs(MLA_ROPE)
    r = MLA_ROPE // 2
    one = jnp.ones((seq, HALF - r), jnp.float32)
    zero = jnp.zeros((seq, HALF - r), jnp.float32)
    cos_mla = jnp.concatenate([cm, one, cm, one], axis=1)
    sin_mla = jnp.concatenate([-sm, zero, sm, zero], axis=1)
    c2, s2 = cs(SWA_HEAD_DIM)
    cos_swa = jnp.concatenate([c2, c2, c2, c2], axis=1)
    sin_swa = jnp.concatenate([-s2, -s2, s2, s2], axis=1)
    return cos_mla, sin_mla, cos_swa, sin_swa


def _rms(v, g):
    y = v * lax.rsqrt(jnp.mean(v * v, axis=-1, keepdims=True) + NORM_EPS)
    return y * g


def _rot(t, c, s):
    return t * c + pltpu.roll(t, HALF, 1) * s


def _dot(a, b):
    return jnp.dot(a, b, preferred_element_type=jnp.float32)


def _dot_nt(a, b):
    return lax.dot_general(a, b, (((1,), (1,)), ((), ())),
                           preferred_element_type=jnp.float32)


def _proj_kernel(x_ref, gpre_ref, win_ref, gqa_ref, wqb_ref, gkva_ref, wkvb_ref,
                 cq_ref, sq_ref, ck_ref, sk_ref, csq_ref, ssq_ref, csk_ref, ssk_ref,
                 q_out, k_out, v_out, swq_out, swk_out, swv_out, gate_out):
    bf = jnp.bfloat16
    hb = _rms(x_ref[...], gpre_ref[...]).astype(bf)
    lane = lax.broadcasted_iota(jnp.int32, (x_ref.shape[0], LANES), 1)
    low = lane < HALF

    qn = _rms(_dot(hb, win_ref[:, C_QA:C_KVA]), gqa_ref[...]).astype(bf)
    qm = _dot(qn, wqb_ref[...])
    cq, sq = cq_ref[...], sq_ref[...]
    for h in range(MLA_HEADS):
        sl = slice(h * LANES, (h + 1) * LANES)
        q_out[:, sl] = _rot(qm[:, sl], cq, sq).astype(bf)

    kvn = _rms(_dot(hb, win_ref[:, C_KVA:C_KR]), gkva_ref[...]).astype(bf)
    kr = _rot(_dot(hb, win_ref[:, C_KR:C_SQ]), ck_ref[...], sk_ref[...])
    kvm = _dot(kvn, wkvb_ref[...])
    nk = MLA_HEADS * LANES
    for h in range(MLA_HEADS):
        sl = slice(h * LANES, (h + 1) * LANES)
        k_out[:, sl] = (kvm[:, sl] + kr).astype(bf)
        vs = slice(nk + h * LANES, nk + (h + 1) * LANES)
        v_out[:, sl] = jnp.where(low, kvm[:, vs], 1.0).astype(bf)

    sqz = _dot(hb, win_ref[:, C_SQ:C_SK])
    csq, ssq = csq_ref[...], ssq_ref[...]
    for j in range(SWA_Q_HEADS // 2):
        sl = slice(j * LANES, (j + 1) * LANES)
        swq_out[:, sl] = _rot(sqz[:, sl], csq, ssq).astype(bf)

    skz = _dot(hb, win_ref[:, C_SK:C_SV])
    svz = _dot(hb, win_ref[:, C_SV:C_GATE])
    csk, ssk = csk_ref[...], ssk_ref[...]
    for g in range(SWA_KV_HEADS):
        sl = slice(g * LANES, (g + 1) * LANES)
        swk_out[:, sl] = _rot(skz[:, sl], csk, ssk).astype(bf)
        swv_out[:, sl] = jnp.where(low, svz[:, sl], 1.0).astype(bf)

    gate_out[...] = jax.nn.sigmoid(_dot(hb, win_ref[:, C_GATE:C_END])).astype(bf)


def _const_spec(shape):
    nd = len(shape)
    return pl.BlockSpec(shape, lambda *_: (0,) * nd, pipeline_mode=pl.Buffered(1))


def _proj_call(x2, gpre, win, gqa, wqb, gkva, wkvb, tables, seq):
    t = x2.shape[0]
    tm = TM_PROJ
    nseq = seq // tm
    row = lambda i: (i, 0)
    tab = pl.BlockSpec((tm, LANES), lambda i: (i % nseq, 0))
    bf = jnp.bfloat16
    widths = (MLA_HEADS * LANES, MLA_HEADS * LANES, MLA_HEADS * LANES,
              SWA_Q_HEADS * SWA_HEAD_DIM, SWA_KV_HEADS * LANES, SWA_KV_HEADS * LANES,
              2 * D_MODEL)
    return pl.pallas_call(
        _proj_kernel,
        grid=(t // tm,),
        in_specs=[pl.BlockSpec((tm, D_MODEL), row),
                  _const_spec(gpre.shape), _const_spec(win.shape),
                  _const_spec(gqa.shape), _const_spec(wqb.shape),
                  _const_spec(gkva.shape), _const_spec(wkvb.shape)] + [tab] * 8,
        out_specs=[pl.BlockSpec((tm, w), row) for w in widths],
        out_shape=[jax.ShapeDtypeStruct((t, w), bf) for w in widths],
        compiler_params=pltpu.CompilerParams(
            dimension_semantics=("parallel",), vmem_limit_bytes=VMEM_LIMIT),
        name="proj",
    )(x2, gpre, win, gqa, wqb, gkva, wkvb, *tables)


def _mla_kernel(q_ref, k_ref, v_ref, o_ref, m_ref, acc_ref, *, tq):
    i = pl.program_id(2)
    row = lax.broadcasted_iota(jnp.int32, (tq, tq), 0)
    col = lax.broadcasted_iota(jnp.int32, (tq, tq), 1)
    causal = col <= row
    outs = []
    for hh in range(2):
        sl = slice(hh * LANES, (hh + 1) * LANES)
        q = q_ref[0, :, sl]
        m_ref[...] = jnp.full(m_ref.shape, NEG, jnp.float32)
        acc_ref[...] = jnp.zeros(acc_ref.shape, jnp.float32)

        def step(s, vb):
            m_prev = m_ref[...]
            m_new = jnp.maximum(m_prev, jnp.max(s, axis=-1, keepdims=True))
            p = jnp.exp(s - m_new)
            acc_ref[...] = jnp.exp(m_prev - m_new) * acc_ref[...] + _dot(p.astype(vb.dtype), vb)
            m_ref[...] = m_new

        def body(j, carry):
            rows = pl.ds(pl.multiple_of(j * tq, tq), tq)
            step(_dot_nt(q, k_ref[0, rows, sl]), v_ref[0, rows, sl])
            return carry

        lax.fori_loop(0, i, body, 0)
        rows = pl.ds(pl.multiple_of(i * tq, tq), tq)
        s = _dot_nt(q, k_ref[0, rows, sl])
        step(jnp.where(causal, s, NEG), v_ref[0, rows, sl])
        acc = acc_ref[...]
        outs.append(acc / pltpu.roll(acc, HALF, 1))
    lane = lax.broadcasted_iota(jnp.int32, (tq, LANES), 1)
    o_ref[0] = jnp.where(lane < HALF, outs[0], pltpu.roll(outs[1], HALF, 1)).astype(o_ref.dtype)


def _mla_call(q, k, v, batch, seq):
    tq = TQ_MLA
    q3 = q.reshape(batch, seq, MLA_HEADS * LANES)
    k3 = k.reshape(batch, seq, MLA_HEADS * LANES)
    v3 = v.reshape(batch, seq, MLA_HEADS * LANES)
    kv_spec = pl.BlockSpec((1, seq, 2 * LANES), lambda b, h, i: (b, 0, h))
    return pl.pallas_call(
        functools.partial(_mla_kernel, tq=tq),
        grid=(batch, MLA_HEADS // 2, seq // tq),
        in_specs=[pl.BlockSpec((1, tq, 2 * LANES), lambda b, h, i: (b, i, h)),
                  kv_spec, kv_spec],
        out_specs=pl.BlockSpec((1, tq, LANES), lambda b, h, i: (b, i, h)),
        out_shape=jax.ShapeDtypeStruct((batch, seq, MLA_HEADS * MLA_V), jnp.bfloat16),
        scratch_shapes=[pltpu.VMEM((tq, 1), jnp.float32),
                        pltpu.VMEM((tq, LANES), jnp.float32)],
        compiler_params=pltpu.CompilerParams(
            dimension_semantics=("parallel", "parallel", "arbitrary"),
            vmem_limit_bytes=VMEM_LIMIT),
        name="mla_attention",
    )(q3, k3, v3)


def _swa_kernel(sink_ref, q_ref, kp_ref, kc_ref, vp_ref, vc_ref, o_ref):
    n = pl.program_id(1)
    w = WINDOW
    kk = jnp.concatenate([kp_ref[0], kc_ref[0]], axis=0)
    vv = jnp.concatenate([vp_ref[0], vc_ref[0]], axis=0)
    row = lax.broadcasted_iota(jnp.int32, (2 * w, 2 * w), 0) % w
    col = lax.broadcasted_iota(jnp.int32, (2 * w, 2 * w), 1)
    first = jnp.where(n > 0, 0, w)
    mask = (col > row) & (col <= row + w) & (col >= first)
    top = lax.broadcasted_iota(jnp.int32, (2 * w, 1), 0) < w
    lane = lax.broadcasted_iota(jnp.int32, (w, LANES), 1)
    head0 = (lane % HALF) < (HALF // 2)
    low = lane < HALF
    for j in range(SWA_Q_HEADS // 2):
        g = (2 * j) // SWA_GROUP
        gs = slice(g * LANES, (g + 1) * LANES)
        qp = q_ref[0, :, j * LANES:(j + 1) * LANES]
        zero = jnp.zeros_like(qp)
        qq = jnp.concatenate([jnp.where(head0, qp, zero), jnp.where(head0, zero, qp)], axis=0)
        s = jnp.where(mask, _dot_nt(qq, kk[:, gs]), NEG)
        sink = jnp.where(top, sink_ref[2 * j], sink_ref[2 * j + 1])
        m = jnp.maximum(jnp.max(s, axis=-1, keepdims=True), sink)
        e = jnp.exp(s - m)
        o2 = _dot(e.astype(vv.dtype), vv[:, gs])
        es = jnp.exp(sink - m)
        a, b = o2[:w], o2[w:]
        out0 = a / (pltpu.roll(a, HALF, 1) + es[:w])
        out1 = pltpu.roll(b, HALF, 1) / (b + es[w:])
        o_ref[0, :, j * LANES:(j + 1) * LANES] = jnp.where(low, out0, out1).astype(o_ref.dtype)


def _swa_call(sinks, q, k, v, batch, seq):
    w = WINDOW
    q3 = q.reshape(batch, seq, SWA_Q_HEADS * SWA_HEAD_DIM)
    k3 = k.reshape(batch, seq, SWA_KV_HEADS * LANES)
    v3 = v.reshape(batch, seq, SWA_KV_HEADS * LANES)
    cur = pl.BlockSpec((1, w, SWA_KV_HEADS * LANES), lambda b, n: (b, n, 0))
    prev = pl.BlockSpec((1, w, SWA_KV_HEADS * LANES),
                        lambda b, n: (b, jnp.maximum(n - 1, 0), 0))
    qo = pl.BlockSpec((1, w, SWA_Q_HEADS * SWA_HEAD_DIM), lambda b, n: (b, n, 0))
    return pl.pallas_call(
        _swa_kernel,
        grid=(batch, seq // w),
        in_specs=[pl.BlockSpec(memory_space=pltpu.SMEM), qo, prev, cur, prev, cur],
        out_specs=qo,
        out_shape=jax.ShapeDtypeStruct(q3.shape, jnp.bfloat16),
        compiler_params=pltpu.CompilerParams(
            dimension_semantics=("parallel", "parallel"), vmem_limit_bytes=VMEM_LIMIT),
        name="swa_attention",
    )(sinks, q3, k3, k3, v3, v3)


def _merge_kernel(x_ref, oa_ref, ob_ref, gate_ref, wa_ref, wb_ref, wo_ref, g_ref, o_ref):
    ga = gate_ref[:, :D_MODEL].astype(jnp.float32)
    gb = gate_ref[:, D_MODEL:].astype(jnp.float32)
    y = ga * _dot(oa_ref[...], wa_ref[...]) + gb * _dot(ob_ref[...], wb_ref[...])
    t = _dot(y.astype(jnp.bfloat16), wo_ref[...])
    o_ref[...] = x_ref[...] + _rms(t, g_ref[...])


def _merge_call(x2, oa, ob, gates, wa, wb, wo, gpost):
    t = x2.shape[0]
    tm = TM_MERGE
    row = lambda i: (i, 0)
    return pl.pallas_call(
        _merge_kernel,
        grid=(t // tm,),
        in_specs=[pl.BlockSpec((tm, D_MODEL), row), pl.BlockSpec((tm, D_MODEL), row),
                  pl.BlockSpec((tm, D_MODEL), row), pl.BlockSpec((tm, 2 * D_MODEL), row),
                  _const_spec(wa.shape), _const_spec(wb.shape), _const_spec(wo.shape),
                  _const_spec(gpost.shape)],
        out_specs=pl.BlockSpec((tm, D_MODEL), row),
        out_shape=jax.ShapeDtypeStruct(x2.shape, jnp.float32),
        compiler_params=pltpu.CompilerParams(
            dimension_semantics=("parallel",), vmem_limit_bytes=VMEM_LIMIT),
        name="merge_out",
    )(x2, oa, ob, gates, wa, wb, wo, gpost)


def _mlp_kernel(x_ref, p_ref, gpre_ref, w1_ref, w2_ref, gpost_ref, wpp_ref, gple_ref,
                wpg_ref, o_ref):
    bf = jnp.bfloat16
    x = x_ref[...]
    hb = _rms(x, gpre_ref[...]).astype(bf)
    acc = jnp.zeros(x.shape, jnp.float32)
    for c in range(D_FF // FF_CHUNK):
        cs = slice(c * FF_CHUNK, (c + 1) * FF_CHUNK)
        u = jnp.square(jnp.maximum(_dot(hb, w1_ref[:, cs]), 0.0))
        acc = acc + _dot(u.astype(bf), w2_ref[cs, :])
    x = x + _rms(acc, gpost_ref[...])
    e = _rms(_dot(p_ref[...].astype(bf), wpp_ref[...]), gple_ref[...])
    gate = jax.nn.sigmoid(_dot(x.astype(bf), wpg_ref[...]))
    o_ref[...] = x + gate * e


def _mlp_call(x2, p2, gpre, w1, w2, gpost, wpp, gple, wpg):
    t = x2.shape[0]
    tm = TM_MLP
    row = lambda i: (i, 0)
    return pl.pallas_call(
        _mlp_kernel,
        grid=(t // tm,),
        in_specs=[pl.BlockSpec((tm, D_MODEL), row), pl.BlockSpec((tm, PLE_DIM), row),
                  _const_spec(gpre.shape), _const_spec(w1.shape), _const_spec(w2.shape),
                  _const_spec(gpost.shape), _const_spec(wpp.shape), _const_spec(gple.shape),
                  _const_spec(wpg.shape)],
        out_specs=pl.BlockSpec((tm, D_MODEL), row),
        out_shape=jax.ShapeDtypeStruct(x2.shape, jnp.float32),
        compiler_params=pltpu.CompilerParams(
            dimension_semantics=("parallel",), vmem_limit_bytes=VMEM_LIMIT),
        name="mlp_ple",
    )(x2, p2, gpre, w1, w2, gpost, wpp, gple, wpg)


def kernel(x, p, g_mix_pre, w_in, g_q_a, w_q_b, g_kv_a, w_kv_b, sinks, w_mla_up, w_swa_up,
           w_out, g_mix_post, g_mlp_pre, w_mlp_up, w_mlp_down, g_mlp_post, w_ple, g_ple,
           w_ple_gate):
    batch, seq, _ = x.shape
    depth = w_in.shape[0]
    bf = jnp.bfloat16
    cos_m, sin_m, cos_s, sin_s = _rope_tables(seq)
    q_scale = (MLA_NOPE + MLA_ROPE) ** -0.5
    s_scale = SWA_HEAD_DIM ** -0.5
    tables = (cos_m * q_scale, sin_m * q_scale, cos_m, sin_m,
              cos_s * s_scale, sin_s * s_scale, cos_s, sin_s)
    in_cols, q_cols, kv_cols = _in_proj_columns(), _q_up_columns(), _kv_up_columns()
    row = lambda g: g.reshape(1, -1)

    x2 = x.reshape(batch * seq, D_MODEL)
    for i in range(depth):
        q, k, v, swq, swk, swv, gates = _proj_call(
            x2, row(g_mix_pre[i]), _take_cols(w_in[i], in_cols),
            row(g_q_a[i]), _take_cols(w_q_b[i], q_cols),
            row(g_kv_a[i]), _take_cols(w_kv_b[i], kv_cols), tables, seq)
        o_mla = _mla_call(q, k, v, batch, seq).reshape(batch * seq, -1)
        o_swa = _swa_call(sinks[i], swq, swk, swv, batch, seq).reshape(batch * seq, -1)
        x2 = _merge_call(x2, o_mla, o_swa, gates, w_mla_up[i].astype(bf),
                         w_swa_up[i].astype(bf), w_out[i].astype(bf), row(g_mix_post[i]))
        x2 = _mlp_call(x2, p[i].reshape(batch * seq, PLE_DIM), row(g_mlp_pre[i]),
                       w_mlp_up[i].astype(bf), w_mlp_down[i].astype(bf), row(g_mlp_post[i]),
                       w_ple[i].astype(bf), row(g_ple[i]), w_ple_gate[i].astype(bf))
    return x2.reshape(batch, seq, D_MODEL)
```
